```python
import math
import jax, jax.numpy as jnp
from jax import lax
import numpy as np

D_MODEL = 1024
BATCH = 4
SEQ = 8192
DEPTH = 1

HEAD_DIM = 64
ATTN_WIDTH = D_MODEL // 2
N_ATTN_HEADS = ATTN_WIDTH // HEAD_DIM
SG_WIDTH = D_MODEL // 2
SG_GROUP_DIM = 64
N_SG_GROUPS = SG_WIDTH // SG_GROUP_DIM
SG_CHUNK = 128
MEM_WIDTH = D_MODEL // 2
N_MEM_HEADS = 4
MEM_HEAD_DIM = MEM_WIDTH // N_MEM_HEADS
N_MEM = 256
D_MIX = ATTN_WIDTH + SG_WIDTH + MEM_WIDTH
IN_WIDTHS = (ATTN_WIDTH, ATTN_WIDTH, ATTN_WIDTH, ATTN_WIDTH,
             2 * SG_WIDTH, SG_WIDTH,
             MEM_WIDTH, MEM_WIDTH)
IN_WIDTH = sum(IN_WIDTHS)
DILATED_CONFIGS = ((128, 1), (512, 4), (2048, 16))
BAND_BLOCK = 64
N_REL_BUCKETS = 32
REL_MAX_DISTANCE = 1024
EPS = 1e-6
NEG_BIG = -1e30

kernel_name = "hybrid_dilated_sgu_memory_layer"


def rms_norm(x, g):
    xf = x.astype(jnp.float32)
    y = xf * lax.rsqrt(jnp.mean(xf * xf, axis=-1, keepdims=True) + EPS)
    return (y * g.astype(jnp.float32)).astype(x.dtype)


def layer_norm(x, g, b):
    xf = x.astype(jnp.float32)
    mu = jnp.mean(xf, axis=-1, keepdims=True)
    var = jnp.mean(jnp.square(xf - mu), axis=-1, keepdims=True)
    y = (xf - mu) * lax.rsqrt(var + EPS)
    return (y * g.astype(jnp.float32) + b.astype(jnp.float32)).astype(x.dtype)


def t5_bucket(rel):
    half = N_REL_BUCKETS // 2
    max_exact = half // 2
    ret = (rel > 0).astype(np.int32) * half
    n = np.abs(rel)
    large = max_exact + (np.log(np.maximum(n, 1).astype(np.float32) / max_exact)
                         / math.log(REL_MAX_DISTANCE / max_exact)
                         * (half - max_exact)).astype(np.int32)
    large = np.minimum(large, half - 1)
    return (ret + np.where(n < max_exact, n, large)).astype(np.int32)


def dilated_band_attention(q, k, v, rel_bias, window, dilation):
    B, S, H, Dh = q.shape
    half = window // (2 * dilation)
    blk = BAND_BLOCK
    seg = dilation * blk
    Sp = -(-S // seg) * seg
    pad = Sp - S
    M = Sp // dilation
    nb = M // blk

    def to_blocks(t):
        t = jnp.pad(t, ((0, 0), (0, pad), (0, 0), (0, 0)))
        t = t.reshape(B, M, dilation, H, Dh).transpose(0, 2, 1, 3, 4)
        return t.reshape(B, dilation, nb, blk, H, Dh)

    def neighbours(t):
        tp = jnp.pad(t, ((0, 0), (0, 0), (1, 1), (0, 0), (0, 0), (0, 0)))
        return jnp.concatenate([tp[:, :, :-2], tp[:, :, 1:-1], tp[:, :, 2:]], axis=3)

    qb = to_blocks(q)
    kn = neighbours(to_blocks(k))
    vn = neighbours(to_blocks(v))

    qi = np.arange(blk)[:, None]
    kj = np.arange(3 * blk)[None, :]
    rel = kj - blk - qi
    band = np.abs(rel) <= half
    bucket = t5_bucket(rel * dilation)
    pos = np.arange(M)[None, :] * dilation + np.arange(dilation)[:, None]
    valid = (pos < S).reshape(dilation, nb, blk)
    valid = np.pad(valid, ((0, 0), (1, 1), (0, 0)))
    valid = np.concatenate([valid[:, :-2], valid[:, 1:-1], valid[:, 2:]], axis=2)
    allowed = band[None, None] & valid[:, :, None, :]

    bias = jnp.transpose(rel_bias.astype(jnp.float32)[bucket], (2, 0, 1))
    s = jnp.einsum('brnqhe,brnkhe->brnhqk', qb, kn).astype(jnp.float32) * (Dh ** -0.5)
    s = s + bias
    s = jnp.where(allowed[None, :, :, None], s, NEG_BIG)
    m = jnp.max(s, axis=-1, keepdims=True)
    p = jnp.exp(s - m)
    l = jnp.sum(p, axis=-1, keepdims=True)
    o = jnp.einsum('brnhqk,brnkhe->brnqhe', p / l, vn.astype(jnp.float32))
    lse = jnp.transpose((m + jnp.log(l))[..., 0], (0, 1, 2, 4, 3))

    o = o.reshape(B, dilation, M, H, Dh).transpose(0, 2, 1, 3, 4).reshape(B, Sp, H, Dh)[:, :S]
    lse = lse.reshape(B, dilation, M, H).transpose(0, 2, 1, 3).reshape(B, Sp, H)[:, :S]
    return o, lse


def setup_inputs(seed: int = 0) -> dict:
    key = jax.random.key(seed)
    ks = jax.random.split(key, 14)
    f32 = jnp.float32
    x = jax.random.normal(ks[0], (BATCH, SEQ, D_MODEL), f32)
    mem = jax.random.normal(ks[1], (BATCH, N_MEM, D_MODEL), f32)
    norm_g = 1.0 + 0.02 * jax.random.normal(ks[2], (DEPTH, D_MODEL), f32)
    mem_norm_g = 1.0 + 0.02 * jax.random.normal(ks[3], (DEPTH, D_MODEL), f32)
    w_in = jax.random.normal(ks[4], (DEPTH, D_MODEL, IN_WIDTH), f32) * D_MODEL ** -0.5
    sg_ln_g = 1.0 + 0.02 * jax.random.normal(ks[5], (DEPTH, SG_WIDTH), f32)
    sg_ln_b = 0.02 * jax.random.normal(ks[6], (DEPTH, SG_WIDTH), f32)
    sg_w = jax.random.normal(ks[7], (DEPTH, N_SG_GROUPS, SG_CHUNK, SG_CHUNK), f32) * SG_CHUNK ** -0.5
    sg_b = 1.0 + 0.02 * jax.random.normal(ks[8], (DEPTH, N_SG_GROUPS, SG_CHUNK), f32)
    w_mem_kv = jax.random.normal(ks[9], (DEPTH, D_MODEL, 2 * MEM_WIDTH), f32) * D_MODEL ** -0.5
    w_out = jax.random.normal(ks[10], (DEPTH, D_MIX, D_MODEL), f32) * D_MIX ** -0.5
    rel_bias = 0.1 * jax.random.normal(ks[11], (N_REL_BUCKETS, N_ATTN_HEADS), f32)
    final_norm_g = 1.0 + 0.02 * jax.random.normal(ks[12], (D_MODEL,), f32)
    return {"x": x, "mem": mem, "norm_g": norm_g, "mem_norm_g": mem_norm_g,
            "w_in": w_in, "sg_ln_g": sg_ln_g, "sg_ln_b": sg_ln_b, "sg_w": sg_w,
            "sg_b": sg_b, "w_mem_kv": w_mem_kv, "w_out": w_out,
            "rel_bias": rel_bias, "final_norm_g": final_norm_g}


def reference(x, mem, norm_g, mem_norm_g, w_in, sg_ln_g, sg_ln_b, sg_w, sg_b,
              w_mem_kv, w_out, rel_bias, final_norm_g):
    B, S, _ = x.shape
    split_points = list(np.cumsum(IN_WIDTHS)[:-1])
    for layer in range(DEPTH):
        h = rms_norm(x, norm_g[layer])
        proj = h @ w_in[layer]
        q, k, v, g_att, sg_uv, g_sg, q_mem, g_mem = jnp.split(proj, split_points, axis=-1)

        qh = q.reshape(B, S, N_ATTN_HEADS, HEAD_DIM)
        kh = k.reshape(B, S, N_ATTN_HEADS, HEAD_DIM)
        vh = v.reshape(B, S, N_ATTN_HEADS, HEAD_DIM)
        outs, lses = [], []
        for window, dilation in DILATED_CONFIGS:
            o, l = dilated_band_attention(qh, kh, vh, rel_bias, window, dilation)
            outs.append(o)
            lses.append(l)
        wts = jax.nn.softmax(jnp.stack(lses), axis=0)
        att = jnp.sum(wts[..., None] * jnp.stack(outs), axis=0)
        att = att.reshape(B, S, ATTN_WIDTH).astype(x.dtype)
        y_att = att * jax.nn.silu(g_att)

        uv = jax.nn.gelu(sg_uv)
        u, vv = jnp.split(uv, 2, axis=-1)
        vv = layer_norm(vv, sg_ln_g[layer], sg_ln_b[layer])
        vv = vv.reshape(B, S // SG_CHUNK, SG_CHUNK, N_SG_GROUPS, SG_GROUP_DIM)
        mixed = jnp.einsum('gpq,bcqge->bcpge', sg_w[layer], vv)
        mixed = mixed + jnp.transpose(sg_b[layer])[None, None, :, :, None]
        y_sg = (u * mixed.reshape(B, S, SG_WIDTH)) * jax.nn.silu(g_sg)

        mem_kv = rms_norm(mem, mem_norm_g[layer]) @ w_mem_kv[layer]
        km, vm = jnp.split(mem_kv, 2, axis=-1)
        km = km.reshape(B, N_MEM, N_MEM_HEADS, MEM_HEAD_DIM)
        vm = vm.reshape(B, N_MEM, N_MEM_HEADS, MEM_HEAD_DIM)
        qm = q_mem.reshape(B, S, N_MEM_HEADS, MEM_HEAD_DIM)
        sm = jnp.einsum('bshe,bmhe->bhsm', qm, km).astype(jnp.float32) * (MEM_HEAD_DIM ** -0.5)
        pm = jax.nn.softmax(sm, axis=-1)
        om = jnp.einsum('bhsm,bmhe->bshe', pm, vm.astype(jnp.float32))
        y_mem = om.reshape(B, S, MEM_WIDTH).astype(x.dtype) * jax.nn.silu(g_mem)

        y = jnp.concatenate([y_att, y_sg, y_mem], axis=-1) @ w_out[layer]
        x = x + y
    return rms_norm(x, final_norm_g)
```

```python
import functools
import math

import numpy as np
import jax
import jax.numpy as jnp
from jax import lax
from jax.experimental import pallas as pl
from jax.experimental.pallas import tpu as pltpu

F32 = jnp.float32
BF16 = jnp.bfloat16

HEAD_DIM = 64
N_ATTN_HEADS = 8
ATTN_WIDTH = 512
SG_WIDTH = 512
SG_CHUNK = 128
MEM_WIDTH = 512
N_MEM_HEADS = 4
MEM_HEAD_DIM = 128
DILATED_CONFIGS = ((128, 1), (512, 4), (2048, 16))
BAND_BLOCK = 64
N_REL_BUCKETS = 32
REL_MAX_DISTANCE = 1024
EPS = 1e-6
NEG_BIG = -1e30

LANES = 128
T_IN = 256
T_ATT = 1024
T_OUT = 512
VMEM_LIMIT = 48 * 1024 * 1024


def _silu(g):
    return g * (1.0 / (1.0 + jnp.exp(-g)))


def _gelu_tanh(x):
    c = math.sqrt(2.0 / math.pi)
    return x * (0.5 * (1.0 + jnp.tanh(c * (x + 0.044715 * (x * x * x)))))


def _dot(a, b):
    return jnp.dot(a, b, preferred_element_type=F32)


def _dot_nt(a, b):
    return lax.dot_general(a, b, (((1,), (1,)), ((), ())), preferred_element_type=F32)


def _mem_kv_kernel(mem_ref, g_ref, w_ref, km_ref, vm_ref):
    m = mem_ref[0]
    h = (m * lax.rsqrt(jnp.mean(m * m, axis=-1, keepdims=True) + EPS)) * g_ref[...]
    kv = _dot(h.astype(BF16), w_ref[...])
    km_ref[0] = kv[:, :MEM_WIDTH].astype(BF16)
    vm_ref[0] = kv[:, MEM_WIDTH:].astype(BF16)


def _in_proj_kernel(x_ref, ng_ref, w_ref, lng_ref, lnb_ref, sgw_ref, sgb_ref, km_ref, vm_ref,
                    q1_ref, q4_ref, q16_ref, k1_ref, k4_ref, k16_ref, v1_ref, v4_ref, v16_ref,
                    gatt_ref, ysg_ref, ymem_ref, scr_ref):
    t = x_ref.shape[1]
    x = x_ref[0]
    h = (x * lax.rsqrt(jnp.mean(x * x, axis=-1, keepdims=True) + EPS)) * ng_ref[...]
    hb = h.astype(BF16)

    outs = ((q1_ref, q4_ref, q16_ref), (k1_ref, k4_ref, k16_ref), (v1_ref, v4_ref, v16_ref))
    for seg in range(3):
        p = _dot(hb, w_ref[:, seg * ATTN_WIDTH:(seg + 1) * ATTN_WIDTH])
        if seg == 0:
            p = p * (HEAD_DIM ** -0.5)
        o1, o4, o16 = outs[seg]
        o1[0] = p.astype(BF16)
        for c in range(ATTN_WIDTH // LANES):
            cols = slice(c * LANES, (c + 1) * LANES)
            scr_ref[c] = p[:, cols]
            for r in range(4):
                o4[0, r, :, cols] = scr_ref[c, pl.ds(r, t // 4, stride=4), :].astype(BF16)
            for r in range(16):
                o16[0, r, :, cols] = scr_ref[c, pl.ds(r, t // 16, stride=16), :].astype(BF16)

    c0 = 3 * ATTN_WIDTH
    gatt_ref[0] = _silu(_dot(hb, w_ref[:, c0:c0 + ATTN_WIDTH]))

    c0 += ATTN_WIDTH
    u = _gelu_tanh(_dot(hb, w_ref[:, c0:c0 + SG_WIDTH]))
    vv = _gelu_tanh(_dot(hb, w_ref[:, c0 + SG_WIDTH:c0 + 2 * SG_WIDTH]))
    mu = jnp.mean(vv, axis=-1, keepdims=True)
    var = jnp.mean(jnp.square(vv - mu), axis=-1, keepdims=True)
    vv = ((vv - mu) * lax.rsqrt(var + EPS)) * lng_ref[...] + lnb_ref[...]
    c0 += 2 * SG_WIDTH
    gs = _silu(_dot(hb, w_ref[:, c0:c0 + SG_WIDTH]))
    lo_half = lax.broadcasted_iota(jnp.int32, (SG_CHUNK, LANES), 1) < HEAD_DIM
    for c in range(t // SG_CHUNK):
        rows = slice(c * SG_CHUNK, (c + 1) * SG_CHUNK)
        for p in range(SG_WIDTH // LANES):
            cols = slice(p * LANES, (p + 1) * LANES)
            vp = vv[rows, cols]
            rhs = jnp.concatenate([jnp.where(lo_half, vp, 0.0).astype(BF16),
                                   jnp.where(lo_half, 0.0, vp).astype(BF16)], axis=0)
            mixed = _dot(sgw_ref[p], rhs) + sgb_ref[:, cols]
            ysg_ref[0, rows, cols] = ((u[rows, cols] * mixed) * gs[rows, cols]).astype(BF16)

    c0 += SG_WIDTH
    qm = _dot(hb, w_ref[:, c0:c0 + MEM_WIDTH]).astype(BF16)
    gm = _silu(_dot(hb, w_ref[:, c0 + MEM_WIDTH:c0 + 2 * MEM_WIDTH]))
    for hd in range(N_MEM_HEADS):
        cols = slice(hd * MEM_HEAD_DIM, (hd + 1) * MEM_HEAD_DIM)
        s = _dot_nt(qm[:, cols], km_ref[0, :, cols]) * (MEM_HEAD_DIM ** -0.5)
        m = jnp.max(s, axis=-1, keepdims=True)
        p = jnp.exp(s - m)
        l = jnp.sum(p, axis=-1, keepdims=True)
        o = _dot(p.astype(BF16), vm_ref[0, :, cols]) / l
        ymem_ref[0, :, cols] = (o * gm[:, cols]).astype(BF16)


def _dilated_kernel(q1, k1p, k1c, k1n, v1p, v1c, v1n,
                    q4, k4p, k4c, k4n, v4p, v4c, v4n,
                    q16, k16p, k16c, k16n, v16p, v16c, v16n,
                    bias_ref, gatt_ref, out_ref, kbuf, vbuf, m_scr, l_scr, a_scr):
    i = pl.program_id(1)
    nt = pl.num_programs(1)
    blk = BAND_BLOCK
    nkeys = 3 * blk
    lo_half = lax.broadcasted_iota(jnp.int32, (blk, LANES), 1) < HEAD_DIM
    col = lax.broadcasted_iota(jnp.int32, (blk, nkeys), 1)
    cfgs = ((1, q1, (k1p, k1c, k1n), (v1p, v1c, v1n)),
            (4, q4, (k4p, k4c, k4n), (v4p, v4c, v4n)),
            (16, q16, (k16p, k16c, k16n), (v16p, v16c, v16n)))

    for ci, (d, q_ref, ks, vs) in enumerate(cfgs):
        mt = T_ATT // d
        nb = mt // blk
        rr = mt + 2 * blk
        for r in range(d):
            for buf, (prv, cur, nxt) in ((kbuf, ks), (vbuf, vs)):
                buf[r * rr:r * rr + blk, :] = prv[0, r]
                buf[r * rr + blk:r * rr + blk + mt, :] = cur[0, r]
                buf[r * rr + blk + mt:(r + 1) * rr, :] = nxt[0, r]

        def body(it, carry, ci=ci, d=d, q_ref=q_ref, nb=nb, rr=rr):
            r = it // nb
            j = it % nb
            base = pl.multiple_of(r * rr + j * blk, blk)
            kb = kbuf[pl.ds(base, nkeys), :]
            vb = vbuf[pl.ds(base, nkeys), :]
            qb = q_ref[0, r, pl.ds(pl.multiple_of(j * blk, blk), blk), :]
            zero = jnp.zeros_like(qb)
            qq = jnp.concatenate([jnp.where(lo_half, qb, zero), jnp.where(lo_half, zero, qb)], axis=0)
            s = _dot_nt(qq, kb)
            lo = jnp.where(jnp.logical_and(i == 0, j == 0), blk, 0)
            hi = jnp.where(jnp.logical_and(i == nt - 1, j == nb - 1), 2 * blk, nkeys)
            valid = jnp.logical_and(col >= lo, col < hi)
            ms, ls, ps = [], [], []
            for hh in range(2):
                sh = jnp.where(valid, s[hh * blk:(hh + 1) * blk] + bias_ref[ci, hh], NEG_BIG)
                mh = jnp.max(sh, axis=-1, keepdims=True)
                ph = jnp.exp(sh - mh)
                ms.append(mh)
                ls.append(jnp.sum(ph, axis=-1, keepdims=True))
                ps.append(ph.astype(BF16))
            o = _dot(jnp.concatenate(ps, axis=0), vb)
            acc = jnp.where(lo_half, o[:blk], o[blk:])
            mb = jnp.where(lo_half, ms[0], ms[1])
            lb = jnp.where(lo_half, ls[0], ls[1])
            start = j * (blk * d) + r
            if d == 1:
                idx = pl.ds(pl.multiple_of(start, blk), blk)
            else:
                idx = pl.ds(start, blk, stride=d)
            if ci == 0:
                m_scr[idx, :] = mb
                l_scr[idx, :] = lb
                a_scr[idx, :] = acc
            else:
                mo = m_scr[idx, :]
                mn = jnp.maximum(mo, mb)
                wa = jnp.exp(mo - mn)
                wb = jnp.exp(mb - mn)
                m_scr[idx, :] = mn
                l_scr[idx, :] = wa * l_scr[idx, :] + wb * lb
                a_scr[idx, :] = wa * a_scr[idx, :] + wb * acc
            return carry

        lax.fori_loop(0, T_ATT // blk, body, 0)

    out_ref[0] = ((a_scr[...] / l_scr[...]) * gatt_ref[0]).astype(BF16)


def _out_proj_kernel(att_ref, ysg_ref, ymem_ref, x_ref, w_ref, g_ref, out_ref):
    y = _dot(att_ref[0], w_ref[0:ATTN_WIDTH, :])
    y = y + _dot(ysg_ref[0], w_ref[ATTN_WIDTH:ATTN_WIDTH + SG_WIDTH, :])
    y = y + _dot(ymem_ref[0], w_ref[ATTN_WIDTH + SG_WIDTH:, :])
    z = x_ref[0] + y
    out_ref[0] = (z * lax.rsqrt(jnp.mean(z * z, axis=-1, keepdims=True) + EPS)) * g_ref[...]


def _t5_bucket(rel):
    half = N_REL_BUCKETS // 2
    max_exact = half // 2
    ret = (rel > 0).astype(np.int32) * half
    n = np.abs(rel)
    large = max_exact + (np.log(np.maximum(n, 1).astype(np.float32) / max_exact)
                         / math.log(REL_MAX_DISTANCE / max_exact)
                         * (half - max_exact)).astype(np.int32)
    large = np.minimum(large, half - 1)
    return (ret + np.where(n < max_exact, n, large)).astype(np.int32)


def _bias_tables(rel_bias):
    blk = BAND_BLOCK
    qi = np.arange(blk)[:, None]
    kj = np.arange(3 * blk)[None, :]
    rel = kj - blk - qi
    tabs = []
    for window, d in DILATED_CONFIGS:
        band = np.abs(rel) <= window // (2 * d)
        b = jnp.transpose(rel_bias.astype(F32)[_t5_bucket(rel * d)], (2, 0, 1))
        tabs.append(jnp.where(band[None], b, NEG_BIG))
    return jnp.stack(tabs)


def _const_spec(shape):
    n = len(shape)
    return pl.BlockSpec(shape, lambda *_: (0,) * n)


def _params(n_grid):
    return pltpu.CompilerParams(dimension_semantics=("arbitrary",) * n_grid,
                                vmem_limit_bytes=VMEM_LIMIT)


def kernel(x, mem, norm_g, mem_norm_g, w_in, sg_ln_g, sg_ln_b, sg_w, sg_b, w_mem_kv, w_out,
           rel_bias, final_norm_g):
    B, S, D = x.shape
    n_mem = mem.shape[1]
    assert w_in.shape[0] == 1, "single-layer problem"
    bias = _bias_tables(rel_bias)
    n_pairs = ATTN_WIDTH // LANES

    for layer in range(1):
        km, vm = pl.pallas_call(
            _mem_kv_kernel,
            grid=(B,),
            in_specs=[pl.BlockSpec((1, n_mem, D), lambda b: (b, 0, 0)),
                      _const_spec((1, D)),
                      _const_spec((D, 2 * MEM_WIDTH))],
            out_specs=[pl.BlockSpec((1, n_mem, MEM_WIDTH), lambda b: (b, 0, 0))] * 2,
            out_shape=[jax.ShapeDtypeStruct((B, n_mem, MEM_WIDTH), BF16)] * 2,
            compiler_params=_params(1),
            name="mem_kv",
        )(mem, mem_norm_g[layer][None], w_mem_kv[layer].astype(BF16))

        t = T_IN
        sgw_pairs = jnp.concatenate([sg_w[layer, 0::2], sg_w[layer, 1::2]], axis=-1).astype(BF16)
        sgb_full = jnp.repeat(jnp.transpose(sg_b[layer]), HEAD_DIM, axis=1)
        tok = lambda w: pl.BlockSpec((1, t, w), lambda b, i: (b, i, 0))
        perm = lambda d: pl.BlockSpec((1, d, t // d, ATTN_WIDTH), lambda b, i: (b, 0, i, 0))
        qkv_shapes = []
        qkv_specs = []
        for _ in range(3):
            qkv_shapes += [jax.ShapeDtypeStruct((B, S, ATTN_WIDTH), BF16),
                           jax.ShapeDtypeStruct((B, 4, S // 4, ATTN_WIDTH), BF16),
                           jax.ShapeDtypeStruct((B, 16, S // 16, ATTN_WIDTH), BF16)]
            qkv_specs += [tok(ATTN_WIDTH), perm(4), perm(16)]
        outs = pl.pallas_call(
            _in_proj_kernel,
            grid=(B, S // t),
            in_specs=[tok(D),
                      _const_spec((1, D)),
                      _const_spec((D, w_in.shape[2])),
                      _const_spec((1, SG_WIDTH)),
                      _const_spec((1, SG_WIDTH)),
                      _const_spec((SG_WIDTH // LANES, SG_CHUNK, 2 * SG_CHUNK)),
                      _const_spec((SG_CHUNK, SG_WIDTH)),
                      pl.BlockSpec((1, n_mem, MEM_WIDTH), lambda b, i: (b, 0, 0)),
                      pl.BlockSpec((1, n_mem, MEM_WIDTH), lambda b, i: (b, 0, 0))],
            out_specs=qkv_specs + [tok(ATTN_WIDTH), tok(SG_WIDTH), tok(MEM_WIDTH)],
            out_shape=qkv_shapes + [jax.ShapeDtypeStruct((B, S, ATTN_WIDTH), F32),
                                    jax.ShapeDtypeStruct((B, S, SG_WIDTH), BF16),
                                    jax.ShapeDtypeStruct((B, S, MEM_WIDTH), BF16)],
            scratch_shapes=[pltpu.VMEM((ATTN_WIDTH // LANES, t, LANES), F32)],
            compiler_params=_params(2),
            name="in_proj",
        )(x, norm_g[layer][None], w_in[layer].astype(BF16), sg_ln_g[layer][None], sg_ln_b[layer][None],
          sgw_pairs, sgb_full, km, vm)
        q1, q4, q16, k1, k4, k16, v1, v4, v16, gatt, ysg, ymem = outs

        blk = BAND_BLOCK
        att_in, att_specs = [], []
        for d, (qa, ka, va) in zip((1, 4, 16), ((q1, k1, v1), (q4, k4, v4), (q16, k16, v16))):
            mt = T_ATT // d
            nb = mt // blk
            last = S // d // blk - 1
            cur = pl.BlockSpec((1, d, mt, LANES), lambda b, i, p: (b, 0, i, p))
            prv = pl.BlockSpec((1, d, blk, LANES),
                               lambda b, i, p, nb=nb: (b, 0, jnp.maximum(i * nb - 1, 0), p))
            nxt = pl.BlockSpec((1, d, blk, LANES),
                               lambda b, i, p, nb=nb, last=last: (b, 0, jnp.minimum((i + 1) * nb, last), p))
            qa, ka, va = (a.reshape(B, d, S // d, ATTN_WIDTH) for a in (qa, ka, va))
            att_in += [qa, ka, ka, ka, va, va, va]
            att_specs += [cur, prv, cur, nxt, prv, cur, nxt]
        att = pl.pallas_call(
            _dilated_kernel,
            grid=(B, S // T_ATT, n_pairs),
            in_specs=att_specs + [
                pl.BlockSpec((len(DILATED_CONFIGS), 2, blk, 3 * blk), lambda b, i, p: (0, p, 0, 0)),
                pl.BlockSpec((1, T_ATT, LANES), lambda b, i, p: (b, i, p))],
            out_specs=pl.BlockSpec((1, T_ATT, LANES), lambda b, i, p: (b, i, p)),
            out_shape=jax.ShapeDtypeStruct((B, S, ATTN_WIDTH), BF16),
            scratch_shapes=[pltpu.VMEM((T_ATT + 2 * blk * 16, LANES), BF16),
                            pltpu.VMEM((T_ATT + 2 * blk * 16, LANES), BF16),
                            pltpu.VMEM((T_ATT, LANES), F32),
                            pltpu.VMEM((T_ATT, LANES), F32),
                            pltpu.VMEM((T_ATT, LANES), F32)],
            compiler_params=_params(3),
            name="dilated",
        )(*att_in, bias, gatt)

        t = T_OUT
        tok = lambda w: pl.BlockSpec((1, t, w), lambda b, i: (b, i, 0))
        x = pl.pallas_call(
            _out_proj_kernel,
            grid=(B, S // t),
            in_specs=[tok(ATTN_WIDTH), tok(SG_WIDTH), tok(MEM_WIDTH), tok(D),
                      _const_spec((w_out.shape[1], D)),
                      _const_spec((1, D))],
            out_specs=tok(D),
            out_shape=jax.ShapeDtypeStruct((B, S, D), F32),
            compiler_params=_params(2),
            name="out_proj",
        )(att, ysg, ymem, x, w_out[layer].astype(BF16), final_norm_g[None])
    return x
```

```python
import functools
import math

import numpy as np
import jax
import jax.numpy as jnp
from jax import lax
from jax.experimental import pallas as pl
from jax.experimental.pallas import tpu as pltpu

F32 = jnp.float32
BF16 = jnp.bfloat16

HEAD_DIM = 64
N_ATTN_HEADS = 8
ATTN_WIDTH = 512
SG_WIDTH = 512
SG_CHUNK = 128
MEM_WIDTH = 512
N_MEM_HEADS = 4
MEM_HEAD_DIM = 128
DILATED_CONFIGS = ((128, 1), (512, 4), (2048, 16))
BAND_BLOCK = 64
N_REL_BUCKETS = 32
REL_MAX_DISTANCE = 1024
EPS = 1e-6
NEG_BIG = -1e30

LANES = 128
T_IN = 256
T_ATT = 1024
ATT_GROUP = 4
T_OUT = 512
VMEM_LIMIT = 48 * 1024 * 1024


def _silu(g):
    return g * (1.0 / (1.0 + jnp.exp(-g)))


def _gelu_tanh(x):
    c = math.sqrt(2.0 / math.pi)
    return x * (0.5 * (1.0 + jnp.tanh(c * (x + 0.044715 * (x * x * x)))))


def _dot(a, b):
    return jnp.dot(a, b, preferred_element_type=F32)


def _dot_nt(a, b):
    return lax.dot_general(a, b, (((1,), (1,)), ((), ())), preferred_element_type=F32)


def _mem_kv_kernel(mem_ref, g_ref, w_ref, km_ref, vm_ref):
    m = mem_ref[0]
    h = (m * lax.rsqrt(jnp.mean(m * m, axis=-1, keepdims=True) + EPS)) * g_ref[...]
    kv = _dot(h.astype(BF16), w_ref[...])
    km_ref[0] = kv[:, :MEM_WIDTH].astype(BF16)
    vm_ref[0] = kv[:, MEM_WIDTH:].astype(BF16)


def _in_proj_kernel(x_ref, ng_ref, w_ref, lng_ref, lnb_ref, sgw_ref, sgb_ref, km_ref, vm_ref,
                    q1_ref, q4_ref, q16_ref, k1_ref, k4_ref, k16_ref, v1_ref, v4_ref, v16_ref,
                    gatt_ref, ysg_ref, ymem_ref, scr_ref):
    t = x_ref.shape[1]
    x = x_ref[0]
    h = (x * lax.rsqrt(jnp.mean(x * x, axis=-1, keepdims=True) + EPS)) * ng_ref[...]
    hb = h.astype(BF16)

    outs = ((q1_ref, q4_ref, q16_ref), (k1_ref, k4_ref, k16_ref), (v1_ref, v4_ref, v16_ref))
    for seg in range(3):
        p = _dot(hb, w_ref[:, seg * ATTN_WIDTH:(seg + 1) * ATTN_WIDTH])
        if seg == 0:
            p = p * (HEAD_DIM ** -0.5)
        o1, o4, o16 = outs[seg]
        o1[0] = p.astype(BF16)
        for c in range(ATTN_WIDTH // LANES):
            cols = slice(c * LANES, (c + 1) * LANES)
            scr_ref[c] = p[:, cols]
            for r in range(4):
                o4[0, r, :, cols] = scr_ref[c, pl.ds(r, t // 4, stride=4), :].astype(BF16)
            for r in range(16):
                o16[0, r, :, cols] = scr_ref[c, pl.ds(r, t // 16, stride=16), :].astype(BF16)

    c0 = 3 * ATTN_WIDTH
    gatt_ref[0] = _silu(_dot(hb, w_ref[:, c0:c0 + ATTN_WIDTH]))

    c0 += ATTN_WIDTH
    u = _gelu_tanh(_dot(hb, w_ref[:, c0:c0 + SG_WIDTH]))
    vv = _gelu_tanh(_dot(hb, w_ref[:, c0 + SG_WIDTH:c0 + 2 * SG_WIDTH]))
    mu = jnp.mean(vv, axis=-1, keepdims=True)
    var = jnp.mean(jnp.square(vv - mu), axis=-1, keepdims=True)
    vv = ((vv - mu) * lax.rsqrt(var + EPS)) * lng_ref[...] + lnb_ref[...]
    c0 += 2 * SG_WIDTH
    gs = _silu(_dot(hb, w_ref[:, c0:c0 + SG_WIDTH]))
    lo_half = lax.broadcasted_iota(jnp.int32, (SG_CHUNK, LANES), 1) < HEAD_DIM
    for c in range(t // SG_CHUNK):
        rows = slice(c * SG_CHUNK, (c + 1) * SG_CHUNK)
        for p in range(SG_WIDTH // LANES):
            cols = slice(p * LANES, (p + 1) * LANES)
            vp = vv[rows, cols]
            rhs = jnp.concatenate([jnp.where(lo_half, vp, 0.0).astype(BF16),
                                   jnp.where(lo_half, 0.0, vp).astype(BF16)], axis=0)
            mixed = _dot(sgw_ref[p], rhs) + sgb_ref[:, cols]
            ysg_ref[0, rows, cols] = ((u[rows, cols] * mixed) * gs[rows, cols]).astype(BF16)

    c0 += SG_WIDTH
    qm = _dot(hb, w_ref[:, c0:c0 + MEM_WIDTH]).astype(BF16)
    gm = _silu(_dot(hb, w_ref[:, c0 + MEM_WIDTH:c0 + 2 * MEM_WIDTH]))
    for hd in range(N_MEM_HEADS):
        cols = slice(hd * MEM_HEAD_DIM, (hd + 1) * MEM_HEAD_DIM)
        s = _dot_nt(qm[:, cols], km_ref[0, :, cols]) * (MEM_HEAD_DIM ** -0.5)
        m = jnp.max(s, axis=-1, keepdims=True)
        p = jnp.exp(s - m)
        l = jnp.sum(p, axis=-1, keepdims=True)
        o = _dot(p.astype(BF16), vm_ref[0, :, cols]) / l
        ymem_ref[0, :, cols] = (o * gm[:, cols]).astype(BF16)


def _dilated_kernel(q1, k1p, k1c, k1n, v1p, v1c, v1n,
                    q4, k4p, k4c, k4n, v4p, v4c, v4n,
                    q16, k16p, k16c, k16n, v16p, v16c, v16n,
                    bias_ref, gatt_ref, out_ref, kbuf, vbuf, m_scr, l_scr, a_scr):
    i = pl.program_id(1)
    nt = pl.num_programs(1)
    blk = BAND_BLOCK
    nkeys = 3 * blk
    lo_half = lax.broadcasted_iota(jnp.int32, (blk, LANES), 1) < HEAD_DIM
    col = lax.broadcasted_iota(jnp.int32, (blk, nkeys), 1)
    pen_lo = jnp.where(jnp.logical_and(i == 0, col < blk), NEG_BIG, 0.0)
    pen_hi = jnp.where(jnp.logical_and(i == nt - 1, col >= 2 * blk), NEG_BIG, 0.0)
    cfgs = ((1, q1, (k1p, k1c, k1n), (v1p, v1c, v1n)),
            (4, q4, (k4p, k4c, k4n), (v4p, v4c, v4n)),
            (16, q16, (k16p, k16c, k16n), (v16p, v16c, v16n)))

    for ci, (d, q_ref, ks, vs) in enumerate(cfgs):
        mt = T_ATT // d
        nb = mt // blk
        rr = mt + 2 * blk
        for r in range(d):
            for buf, (prv, cur, nxt) in ((kbuf, ks), (vbuf, vs)):
                buf[r * rr:r * rr + blk, :] = prv[0, r]
                buf[r * rr + blk:r * rr + blk + mt, :] = cur[0, r]
                buf[r * rr + blk + mt:(r + 1) * rr, :] = nxt[0, r]

        def scores(r, j, q_ref=q_ref, rr=rr):
            base = r * rr + j * blk
            qb = q_ref[0, r, j * blk:(j + 1) * blk, :]
            zero = jnp.zeros_like(qb)
            qq = jnp.concatenate([jnp.where(lo_half, qb, zero), jnp.where(lo_half, zero, qb)], axis=0)
            return _dot_nt(qq, kbuf[base:base + nkeys, :])

        def block(s, r, j, ci=ci, nb=nb, rr=rr):
            base = r * rr + j * blk
            vb = vbuf[base:base + nkeys, :]
            ms, ls, ps = [], [], []
            for hh in range(2):
                sh = s[hh * blk:(hh + 1) * blk] + bias_ref[ci, hh]
                if j == 0:
                    sh = sh + pen_lo
                if j == nb - 1:
                    sh = sh + pen_hi
                mh = jnp.max(sh, axis=-1, keepdims=True)
                ph = jnp.exp(sh - mh)
                ms.append(mh)
                ls.append(jnp.sum(ph, axis=-1, keepdims=True))
                ps.append(ph.astype(BF16))
            o = _dot(jnp.concatenate(ps, axis=0), vb)
            return (jnp.where(lo_half, ms[0], ms[1]), jnp.where(lo_half, ls[0], ls[1]),
                    jnp.where(lo_half, o[:blk], o[blk:]))

        blocks = [(r, j) for r in range(d) for j in range(nb)]
        groups = [blocks[g0:g0 + ATT_GROUP] for g0 in range(0, len(blocks), ATT_GROUP)]
        s_next = [scores(r, j) for r, j in groups[0]]
        for gi, grp in enumerate(groups):
            s_cur = s_next
            if gi + 1 < len(groups):
                s_next = [scores(r, j) for r, j in groups[gi + 1]]
            idxs = [pl.ds(j * (blk * d) + r, blk, stride=d) if d > 1 else pl.ds(j * blk, blk)
                    for r, j in grp]
            if ci > 0:
                olds = [(m_scr[idx, :], l_scr[idx, :], a_scr[idx, :]) for idx in idxs]
            news = [block(s, r, j) for s, (r, j) in zip(s_cur, grp)]
            for n, idx in enumerate(idxs):
                mb, lb, acc = news[n]
                if ci > 0:
                    mo, lo_, ao = olds[n]
                    mn = jnp.maximum(mo, mb)
                    wa = jnp.exp(mo - mn)
                    wb = jnp.exp(mb - mn)
                    mb, lb, acc = mn, wa * lo_ + wb * lb, wa * ao + wb * acc
                m_scr[idx, :] = mb
                l_scr[idx, :] = lb
                a_scr[idx, :] = acc

    out_ref[0] = ((a_scr[...] / l_scr[...]) * gatt_ref[0]).astype(BF16)


def _out_proj_kernel(att_ref, ysg_ref, ymem_ref, x_ref, w_ref, g_ref, out_ref):
    y = _dot(att_ref[0], w_ref[0:ATTN_WIDTH, :])
    y = y + _dot(ysg_ref[0], w_ref[ATTN_WIDTH:ATTN_WIDTH + SG_WIDTH, :])
    y = y + _dot(ymem_ref[0], w_ref[ATTN_WIDTH + SG_WIDTH:, :])
    z = x_ref[0] + y
    out_ref[0] = (z * lax.rsqrt(jnp.mean(z * z, axis=-1, keepdims=True) + EPS)) * g_ref[...]


def _t5_bucket(rel):
    half = N_REL_BUCKETS // 2
    max_exact = half // 2
    ret = (rel > 0).astype(np.int32) * half
    n = np.abs(rel)
    large = max_exact + (np.log(np.maximum(n, 1).astype(np.float32) / max_exact)
                         / math.log(REL_MAX_DISTANCE / max_exact)
                         * (half - max_exact)).astype(np.int32)
    large = np.minimum(large, half - 1)
    return (ret + np.where(n < max_exact, n, large)).astype(np.int32)


def _bias_tables(rel_bias):
    blk = BAND_BLOCK
    qi = np.arange(blk)[:, None]
    kj = np.arange(3 * blk)[None, :]
    rel = kj - blk - qi
    tabs = []
    for window, d in DILATED_CONFIGS:
        band = np.abs(rel) <= window // (2 * d)
        b = jnp.transpose(rel_bias.astype(F32)[_t5_bucket(rel * d)], (2, 0, 1))
        tabs.append(jnp.where(band[None], b, NEG_BIG))
    return jnp.stack(tabs)


def _const_spec(shape):
    n = len(shape)
    return pl.BlockSpec(shape, lambda *_: (0,) * n)


def _params(n_grid):
    return pltpu.CompilerParams(dimension_semantics=("arbitrary",) * n_grid,
                                vmem_limit_bytes=VMEM_LIMIT)


def kernel(x, mem, norm_g, mem_norm_g, w_in, sg_ln_g, sg_ln_b, sg_w, sg_b, w_mem_kv, w_out,
           rel_bias, final_norm_g):
    B, S, D = x.shape
    n_mem = mem.shape[1]
    assert w_in.shape[0] == 1, "single-layer problem"
    bias = _bias_tables(rel_bias)
    n_pairs = ATTN_WIDTH // LANES

    for layer in range(1):
        km, vm = pl.pallas_call(
            _mem_kv_kernel,
            grid=(B,),
            in_specs=[pl.BlockSpec((1, n_mem, D), lambda b: (b, 0, 0)),
                      _const_spec((1, D)),
                      _const_spec((D, 2 * MEM_WIDTH))],
            out_specs=[pl.BlockSpec((1, n_mem, MEM_WIDTH), lambda b: (b, 0, 0))] * 2,
            out_shape=[jax.ShapeDtypeStruct((B, n_mem, MEM_WIDTH), BF16)] * 2,
            compiler_params=_params(1),
            name="mem_kv",
        )(mem, mem_norm_g[layer][None], w_mem_kv[layer].astype(BF16))

        t = T_IN
        sgw_pairs = jnp.concatenate([sg_w[layer, 0::2], sg_w[layer, 1::2]], axis=-1).astype(BF16)
        sgb_full = jnp.repeat(jnp.transpose(sg_b[layer]), HEAD_DIM, axis=1)
        tok = lambda w: pl.BlockSpec((1, t, w), lambda b, i: (b, i, 0))
        perm = lambda d: pl.BlockSpec((1, d, t // d, ATTN_WIDTH), lambda b, i: (b, 0, i, 0))
        qkv_shapes = []
        qkv_specs = []
        for _ in range(3):
            qkv_shapes += [jax.ShapeDtypeStruct((B, S, ATTN_WIDTH), BF16),
                           jax.ShapeDtypeStruct((B, 4, S // 4, ATTN_WIDTH), BF16),
                           jax.ShapeDtypeStruct((B, 16, S // 16, ATTN_WIDTH), BF16)]
            qkv_specs += [tok(ATTN_WIDTH), perm(4), perm(16)]
        outs = pl.pallas_call(
            _in_proj_kernel,
            grid=(B, S // t),
            in_specs=[tok(D),
                      _const_spec((1, D)),
                      _const_spec((D, w_in.shape[2])),
                      _const_spec((1, SG_WIDTH)),
                      _const_spec((1, SG_WIDTH)),
                      _const_spec((SG_WIDTH // LANES, SG_CHUNK, 2 * SG_CHUNK)),
                      _const_spec((SG_CHUNK, SG_WIDTH)),
                      pl.BlockSpec((1, n_mem, MEM_WIDTH), lambda b, i: (b, 0, 0)),
                      pl.BlockSpec((1, n_mem, MEM_WIDTH), lambda b, i: (b, 0, 0))],
            out_specs=qkv_specs + [tok(ATTN_WIDTH), tok(SG_WIDTH), tok(MEM_WIDTH)],
            out_shape=qkv_shapes + [jax.ShapeDtypeStruct((B, S, ATTN_WIDTH), F32),
                                    jax.ShapeDtypeStruct((B, S, SG_WIDTH), BF16),
                                    jax.ShapeDtypeStruct((B, S, MEM_WIDTH), BF16)],
            scratch_shapes=[pltpu.VMEM((ATTN_WIDTH // LANES, t, LANES), F32)],
            compiler_params=_params(2),
            name="in_proj",
        )(x, norm_g[layer][None], w_in[layer].astype(BF16), sg_ln_g[layer][None], sg_ln_b[layer][None],
          sgw_pairs, sgb_full, km, vm)
        q1, q4, q16, k1, k4, k16, v1, v4, v16, gatt, ysg, ymem = outs

        blk = BAND_BLOCK
        att_in, att_specs = [], []
        for d, (qa, ka, va) in zip((1, 4, 16), ((q1, k1, v1), (q4, k4, v4), (q16, k16, v16))):
            mt = T_ATT // d
            nb = mt // blk
            last = S // d // blk - 1
            cur = pl.BlockSpec((1, d, mt, LANES), lambda b, i, p: (b, 0, i, p))
            prv = pl.BlockSpec((1, d, blk, LANES),
                               lambda b, i, p, nb=nb: (b, 0, jnp.maximum(i * nb - 1, 0), p))
            nxt = pl.BlockSpec((1, d, blk, LANES),
                               lambda b, i, p, nb=nb, last=last: (b, 0, jnp.minimum((i + 1) * nb, last), p))
            qa, ka, va = (a.reshape(B, d, S // d, ATTN_WIDTH) for a in (qa, ka, va))
            att_in += [qa, ka, ka, ka, va, va, va]
            att_specs += [cur, prv, cur, nxt, prv, cur, nxt]
        att = pl.pallas_call(
            _dilated_kernel,
            grid=(B, S // T_ATT, n_pairs),
            in_specs=att_specs + [
                pl.BlockSpec((len(DILATED_CONFIGS), 2, blk, 3 * blk), lambda b, i, p: (0, p, 0, 0)),
                pl.BlockSpec((1, T_ATT, LANES), lambda b, i, p: (b, i, p))],
            out_specs=pl.BlockSpec((1, T_ATT, LANES), lambda b, i, p: (b, i, p)),
            out_shape=jax.ShapeDtypeStruct((B, S, ATTN_WIDTH), BF16),
            scratch_shapes=[pltpu.VMEM((T_ATT + 2 * blk * 16, LANES), BF16),
                            pltpu.VMEM((T_ATT + 2 * blk * 16, LANES), BF16),
                            pltpu.VMEM((T_ATT, LANES), F32),
                            pltpu.VMEM((T_ATT, LANES), F32),
                            pltpu.VMEM((T_ATT, LANES), F32)],
            compiler_params=_params(3),
            name="dilated",
        )(*att_in, bias, gatt)

        t = T_OUT
        tok = lambda w: pl.BlockSpec((1, t, w), lambda b, i: (b, i, 0))
        x = pl.pallas_call(
            _out_proj_kernel,
            grid=(B, S // t),
            in_specs=[tok(ATTN_WIDTH), tok(SG_WIDTH), tok(MEM_WIDTH), tok(D),
                      _const_spec((w_out.shape[1], D)),
                      _const_spec((1, D))],
            out_specs=tok(D),
            out_shape=jax.ShapeDtypeStruct((B, S, D), F32),
            compiler_params=_params(2),
            name="out_proj",
        )(att, ysg, ymem, x, w_out[layer].astype(BF16), final_norm_g[None])
    return x
```

```python
import functools
import math

import numpy as np
import jax
import jax.numpy as jnp
from jax import lax
from jax.experimental import pallas as pl
from jax.experimental.pallas import tpu as pltpu

F32 = jnp.float32
BF16 = jnp.bfloat16

HEAD_DIM = 64
N_ATTN_HEADS = 8
ATTN_WIDTH = 512
SG_WIDTH = 512
SG_CHUNK = 128
MEM_WIDTH = 512
N_MEM_HEADS = 4
MEM_HEAD_DIM = 128
DILATED_CONFIGS = ((128, 1), (512, 4), (2048, 16))
BAND_BLOCK = 64
N_REL_BUCKETS = 32
REL_MAX_DISTANCE = 1024
EPS = 1e-6
NEG_BIG = -1e30

LANES = 128
T_IN = 512
T_ATT = 1024
ATT_GROUP = 4
T_OUT = 512
VMEM_LIMIT = 48 * 1024 * 1024


def _silu(g):
    return g * (1.0 / (1.0 + jnp.exp(-g)))


def _gelu_tanh(x):
    c = math.sqrt(2.0 / math.pi)
    return x * (0.5 * (1.0 + jnp.tanh(c * (x + 0.044715 * (x * x * x)))))


def _dot(a, b):
    return jnp.dot(a, b, preferred_element_type=F32)


def _dot_nt(a, b):
    return lax.dot_general(a, b, (((1,), (1,)), ((), ())), preferred_element_type=F32)


def _mem_kv_kernel(mem_ref, g_ref, w_ref, km_ref, vm_ref):
    m = mem_ref[0]
    h = (m * lax.rsqrt(jnp.mean(m * m, axis=-1, keepdims=True) + EPS)) * g_ref[...]
    kv = _dot(h.astype(BF16), w_ref[...])
    km_ref[0] = kv[:, :MEM_WIDTH].astype(BF16)
    vm_ref[0] = kv[:, MEM_WIDTH:].astype(BF16)


def _in_proj_kernel(x_ref, ng_ref, w_ref, lng_ref, lnb_ref, sgw_ref, sgb_ref, km_ref, vm_ref,
                    q1_ref, q4_ref, q16_ref, k1_ref, k4_ref, k16_ref, v1_ref, v4_ref, v16_ref,
                    gatt_ref, ysg_ref, ymem_ref, scr_ref):
    t = x_ref.shape[1]
    x = x_ref[0]
    h = (x * lax.rsqrt(jnp.mean(x * x, axis=-1, keepdims=True) + EPS)) * ng_ref[...]
    hb = h.astype(BF16)

    seg_w = ATTN_WIDTH

    def mm(i):
        return _dot(hb, w_ref[:, i * seg_w:(i + 1) * seg_w])

    def put_qkv(seg, p, outs):
        if seg == 0:
            p = p * (HEAD_DIM ** -0.5)
        o1, o4, o16 = outs
        o1[0] = p.astype(BF16)
        for c in range(ATTN_WIDTH // LANES):
            cols = slice(c * LANES, (c + 1) * LANES)
            scr_ref[seg, c] = p[:, cols]
            for r in range(4):
                o4[0, r, :, cols] = scr_ref[seg, c, pl.ds(r, t // 4, stride=4), :].astype(BF16)
            for r in range(16):
                o16[0, r, :, cols] = scr_ref[seg, c, pl.ds(r, t // 16, stride=16), :].astype(BF16)

    p_q = mm(0)
    p_k = mm(1)
    put_qkv(0, p_q, (q1_ref, q4_ref, q16_ref))
    p_v = mm(2)
    put_qkv(1, p_k, (k1_ref, k4_ref, k16_ref))
    p_ga = mm(3)
    put_qkv(2, p_v, (v1_ref, v4_ref, v16_ref))
    p_u = mm(4)
    gatt_ref[0] = _silu(p_ga)
    p_vv = mm(5)
    u = _gelu_tanh(p_u)
    p_gs = mm(6)

    vv = _gelu_tanh(p_vv)
    mu = jnp.mean(vv, axis=-1, keepdims=True)
    var = jnp.mean(jnp.square(vv - mu), axis=-1, keepdims=True)
    vv = ((vv - mu) * lax.rsqrt(var + EPS)) * lng_ref[...] + lnb_ref[...]
    p_qm = mm(7)
    lo_half = lax.broadcasted_iota(jnp.int32, (SG_CHUNK, LANES), 1) < HEAD_DIM
    tiles = [(c, p) for c in range(t // SG_CHUNK) for p in range(SG_WIDTH // LANES)]
    mixed = []
    for c, p in tiles:
        vp = vv[c * SG_CHUNK:(c + 1) * SG_CHUNK, p * LANES:(p + 1) * LANES]
        rhs = jnp.concatenate([jnp.where(lo_half, vp, 0.0).astype(BF16),
                               jnp.where(lo_half, 0.0, vp).astype(BF16)], axis=0)
        mixed.append(_dot(sgw_ref[p], rhs))
    gs = _silu(p_gs)
    p_gm = mm(8)

    qm = p_qm.astype(BF16)
    heads = [slice(hd * MEM_HEAD_DIM, (hd + 1) * MEM_HEAD_DIM) for hd in range(N_MEM_HEADS)]
    s_mem = [_dot_nt(qm[:, cols], km_ref[0, :, cols]) for cols in heads]

    for (c, p), mx in zip(tiles, mixed):
        rows = slice(c * SG_CHUNK, (c + 1) * SG_CHUNK)
        cols = slice(p * LANES, (p + 1) * LANES)
        ysg_ref[0, rows, cols] = ((u[rows, cols] * (mx + sgb_ref[:, cols])) * gs[rows, cols]).astype(BF16)

    gm = _silu(p_gm)
    for cols, s in zip(heads, s_mem):
        s = s * (MEM_HEAD_DIM ** -0.5)
        m = jnp.max(s, axis=-1, keepdims=True)
        p = jnp.exp(s - m)
        l = jnp.sum(p, axis=-1, keepdims=True)
        o = _dot(p.astype(BF16), vm_ref[0, :, cols]) / l
        ymem_ref[0, :, cols] = (o * gm[:, cols]).astype(BF16)


def _dilated_kernel(q1, k1p, k1c, k1n, v1p, v1c, v1n,
                    q4, k4p, k4c, k4n, v4p, v4c, v4n,
                    q16, k16p, k16c, k16n, v16p, v16c, v16n,
                    bias_ref, gatt_ref, out_ref, kbuf, vbuf, m_scr, l_scr, a_scr):
    i = pl.program_id(1)
    nt = pl.num_programs(1)
    blk = BAND_BLOCK
    nkeys = 3 * blk
    lo_half = lax.broadcasted_iota(jnp.int32, (blk, LANES), 1) < HEAD_DIM
    col = lax.broadcasted_iota(jnp.int32, (blk, nkeys), 1)
    pen_lo = jnp.where(jnp.logical_and(i == 0, col < blk), NEG_BIG, 0.0)
    pen_hi = jnp.where(jnp.logical_and(i == nt - 1, col >= 2 * blk), NEG_BIG, 0.0)
    cfgs = ((1, q1, (k1p, k1c, k1n), (v1p, v1c, v1n)),
            (4, q4, (k4p, k4c, k4n), (v4p, v4c, v4n)),
            (16, q16, (k16p, k16c, k16n), (v16p, v16c, v16n)))

    for ci, (d, q_ref, ks, vs) in enumerate(cfgs):
        mt = T_ATT // d
        nb = mt // blk
        rr = mt + 2 * blk
        for r in range(d):
            for buf, (prv, cur, nxt) in ((kbuf, ks), (vbuf, vs)):
                buf[r * rr:r * rr + blk, :] = prv[0, r]
                buf[r * rr + blk:r * rr + blk + mt, :] = cur[0, r]
                buf[r * rr + blk + mt:(r + 1) * rr, :] = nxt[0, r]

        def scores(r, j, q_ref=q_ref, rr=rr):
            base = r * rr + j * blk
            qb = q_ref[0, r, j * blk:(j + 1) * blk, :]
            zero = jnp.zeros_like(qb)
            qq = jnp.concatenate([jnp.where(lo_half, qb, zero), jnp.where(lo_half, zero, qb)], axis=0)
            return _dot_nt(qq, kbuf[base:base + nkeys, :])

        def block(s, r, j, ci=ci, nb=nb, rr=rr):
            base = r * rr + j * blk
            vb = vbuf[base:base + nkeys, :]
            ms, ls, ps = [], [], []
            for hh in range(2):
                sh = s[hh * blk:(hh + 1) * blk] + bias_ref[ci, hh]
                if j == 0:
                    sh = sh + pen_lo
                if j == nb - 1:
                    sh = sh + pen_hi
                mh = jnp.max(sh, axis=-1, keepdims=True)
                ph = jnp.exp(sh - mh)
                ms.append(mh)
                ls.append(jnp.sum(ph, axis=-1, keepdims=True))
                ps.append(ph.astype(BF16))
            o = _dot(jnp.concatenate(ps, axis=0), vb)
            return (jnp.where(lo_half, ms[0], ms[1]), jnp.where(lo_half, ls[0], ls[1]),
                    jnp.where(lo_half, o[:blk], o[blk:]))

        blocks = [(r, j) for r in range(d) for j in range(nb)]
        groups = [blocks[g0:g0 + ATT_GROUP] for g0 in range(0, len(blocks), ATT_GROUP)]
        s_next = [scores(r, j) for r, j in groups[0]]
        for gi, grp in enumerate(groups):
            s_cur = s_next
            if gi + 1 < len(groups):
                s_next = [scores(r, j) for r, j in groups[gi + 1]]
            idxs = [pl.ds(j * (blk * d) + r, blk, stride=d) if d > 1 else pl.ds(j * blk, blk)
                    for r, j in grp]
            if ci > 0:
                olds = [(m_scr[idx, :], l_scr[idx, :], a_scr[idx, :]) for idx in idxs]
            news = [block(s, r, j) for s, (r, j) in zip(s_cur, grp)]
            for n, idx in enumerate(idxs):
                mb, lb, acc = news[n]
                if ci > 0:
                    mo, lo_, ao = olds[n]
                    mn = jnp.maximum(mo, mb)
                    wa = jnp.exp(mo - mn)
                    wb = jnp.exp(mb - mn)
                    mb, lb, acc = mn, wa * lo_ + wb * lb, wa * ao + wb * acc
                m_scr[idx, :] = mb
                l_scr[idx, :] = lb
                a_scr[idx, :] = acc

    out_ref[0] = ((a_scr[...] / l_scr[...]) * gatt_ref[0]).astype(BF16)


def _out_proj_kernel(att_ref, ysg_ref, ymem_ref, x_ref, w_ref, g_ref, out_ref):
    y = _dot(att_ref[0], w_ref[0:ATTN_WIDTH, :])
    y = y + _dot(ysg_ref[0], w_ref[ATTN_WIDTH:ATTN_WIDTH + SG_WIDTH, :])
    y = y + _dot(ymem_ref[0], w_ref[ATTN_WIDTH + SG_WIDTH:, :])
    z = x_ref[0] + y
    out_ref[0] = (z * lax.rsqrt(jnp.mean(z * z, axis=-1, keepdims=True) + EPS)) * g_ref[...]


def _t5_bucket(rel):
    half = N_REL_BUCKETS // 2
    max_exact = half // 2
    ret = (rel > 0).astype(np.int32) * half
    n = np.abs(rel)
    large = max_exact + (np.log(np.maximum(n, 1).astype(np.float32) / max_exact)
                         / math.log(REL_MAX_DISTANCE / max_exact)
                         * (half - max_exact)).astype(np.int32)
    large = np.minimum(large, half - 1)
    return (ret + np.where(n < max_exact, n, large)).astype(np.int32)


def _bias_tables(rel_bias):
    blk = BAND_BLOCK
    qi = np.arange(blk)[:, None]
    kj = np.arange(3 * blk)[None, :]
    rel = kj - blk - qi
    n_cfg = len(DILATED_CONFIGS)
    onehot = np.zeros((n_cfg, blk * 3 * blk, N_REL_BUCKETS), np.float32)
    outside = np.zeros((n_cfg, 1, blk, 3 * blk), np.float32)
    for c, (window, d) in enumerate(DILATED_CONFIGS):
        band = np.abs(rel) <= window // (2 * d)
        onehot[c, np.arange(rel.size), _t5_bucket(rel * d).reshape(-1)] = band.reshape(-1)
        outside[c, 0] = np.where(band, 0.0, NEG_BIG)
    tab = jnp.einsum("cnb,bh->chn", onehot, rel_bias.astype(F32), precision=lax.Precision.HIGHEST)
    return tab.reshape(n_cfg, -1, blk, 3 * blk) + outside


def _const_spec(shape):
    n = len(shape)
    return pl.BlockSpec(shape, lambda *_: (0,) * n)


def _params(n_grid):
    return pltpu.CompilerParams(dimension_semantics=("arbitrary",) * n_grid,
                                vmem_limit_bytes=VMEM_LIMIT)


def kernel(x, mem, norm_g, mem_norm_g, w_in, sg_ln_g, sg_ln_b, sg_w, sg_b, w_mem_kv, w_out,
           rel_bias, final_norm_g):
    B, S, D = x.shape
    n_mem = mem.shape[1]
    assert w_in.shape[0] == 1, "single-layer problem"
    bias = _bias_tables(rel_bias)
    n_pairs = ATTN_WIDTH // LANES

    for layer in range(1):
        km, vm = pl.pallas_call(
            _mem_kv_kernel,
            grid=(B,),
            in_specs=[pl.BlockSpec((1, n_mem, D), lambda b: (b, 0, 0)),
                      _const_spec((1, D)),
                      _const_spec((D, 2 * MEM_WIDTH))],
            out_specs=[pl.BlockSpec((1, n_mem, MEM_WIDTH), lambda b: (b, 0, 0))] * 2,
            out_shape=[jax.ShapeDtypeStruct((B, n_mem, MEM_WIDTH), BF16)] * 2,
            compiler_params=_params(1),
            name="mem_kv",
        )(mem, mem_norm_g[layer][None], w_mem_kv[layer].astype(BF16))

        t = T_IN
        sgw_pairs = jnp.concatenate([sg_w[layer, 0::2], sg_w[layer, 1::2]], axis=-1).astype(BF16)
        sgb_full = jnp.repeat(jnp.transpose(sg_b[layer]), HEAD_DIM, axis=1)
        tok = lambda w: pl.BlockSpec((1, t, w), lambda b, i: (b, i, 0))
        perm = lambda d: pl.BlockSpec((1, d, t // d, ATTN_WIDTH), lambda b, i: (b, 0, i, 0))
        qkv_shapes = []
        qkv_specs = []
        for _ in range(3):
            qkv_shapes += [jax.ShapeDtypeStruct((B, S, ATTN_WIDTH), BF16),
                           jax.ShapeDtypeStruct((B, 4, S // 4, ATTN_WIDTH), BF16),
                           jax.ShapeDtypeStruct((B, 16, S // 16, ATTN_WIDTH), BF16)]
            qkv_specs += [tok(ATTN_WIDTH), perm(4), perm(16)]
        outs = pl.pallas_call(
            _in_proj_kernel,
            grid=(B, S // t),
            in_specs=[tok(D),
                      _const_spec((1, D)),
                      _const_spec((D, w_in.shape[2])),
                      _const_spec((1, SG_WIDTH)),
                      _const_spec((1, SG_WIDTH)),
                      _const_spec((SG_WIDTH // LANES, SG_CHUNK, 2 * SG_CHUNK)),
                      _const_spec((SG_CHUNK, SG_WIDTH)),
                      pl.BlockSpec((1, n_mem, MEM_WIDTH), lambda b, i: (b, 0, 0)),
                      pl.BlockSpec((1, n_mem, MEM_WIDTH), lambda b, i: (b, 0, 0))],
            out_specs=qkv_specs + [tok(ATTN_WIDTH), tok(SG_WIDTH), tok(MEM_WIDTH)],
            out_shape=qkv_shapes + [jax.ShapeDtypeStruct((B, S, ATTN_WIDTH), F32),
                                    jax.ShapeDtypeStruct((B, S, SG_WIDTH), BF16),
                                    jax.ShapeDtypeStruct((B, S, MEM_WIDTH), BF16)],
            scratch_shapes=[pltpu.VMEM((3, ATTN_WIDTH // LANES, t, LANES), F32)],
            compiler_params=_params(2),
            name="in_proj",
        )(x, norm_g[layer][None], w_in[layer].astype(BF16), sg_ln_g[layer][None], sg_ln_b[layer][None],
          sgw_pairs, sgb_full, km, vm)
        q1, q4, q16, k1, k4, k16, v1, v4, v16, gatt, ysg, ymem = outs

        blk = BAND_BLOCK
        att_in, att_specs = [], []
        for d, (qa, ka, va) in zip((1, 4, 16), ((q1, k1, v1), (q4, k4, v4), (q16, k16, v16))):
            mt = T_ATT // d
            nb = mt // blk
            last = S // d // blk - 1
            cur = pl.BlockSpec((1, d, mt, LANES), lambda b, i, p: (b, 0, i, p))
            prv = pl.BlockSpec((1, d, blk, LANES),
                               lambda b, i, p, nb=nb: (b, 0, jnp.maximum(i * nb - 1, 0), p))
            nxt = pl.BlockSpec((1, d, blk, LANES),
                               lambda b, i, p, nb=nb, last=last: (b, 0, jnp.minimum((i + 1) * nb, last), p))
            qa, ka, va = (a.reshape(B, d, S // d, ATTN_WIDTH) for a in (qa, ka, va))
            att_in += [qa, ka, ka, ka, va, va, va]
            att_specs += [cur, prv, cur, nxt, prv, cur, nxt]
        att = pl.pallas_call(
            _dilated_kernel,
            grid=(B, S // T_ATT, n_pairs),
            in_specs=att_specs + [
                pl.BlockSpec((len(DILATED_CONFIGS), 2, blk, 3 * blk), lambda b, i, p: (0, p, 0, 0)),
                pl.BlockSpec((1, T_ATT, LANES), lambda b, i, p: (b, i, p))],
            out_specs=pl.BlockSpec((1, T_ATT, LANES), lambda b, i, p: (b, i, p)),
            out_shape=jax.ShapeDtypeStruct((B, S, ATTN_WIDTH), BF16),
            scratch_shapes=[pltpu.VMEM((T_ATT + 2 * blk * 16, LANES), BF16),
                            pltpu.VMEM((T_ATT + 2 * blk * 16, LANES), BF16),
                            pltpu.VMEM((T_ATT, LANES), F32),
                            pltpu.VMEM((T_ATT, LANES), F32),
                            pltpu.VMEM((T_ATT, LANES), F32)],
            compiler_params=_params(3),
            name="dilated",
        )(*att_in, bias, gatt)

        t = T_OUT
        tok = lambda w: pl.BlockSpec((1, t, w), lambda b, i: (b, i, 0))
        x = pl.pallas_call(
            _out_proj_kernel,
            grid=(B, S // t),
            in_specs=[tok(ATTN_WIDTH), tok(SG_WIDTH), tok(MEM_WIDTH), tok(D),
                      _const_spec((w_out.shape[1], D)),
                      _const_spec((1, D))],
            out_specs=tok(D),
            out_shape=jax.ShapeDtypeStruct((B, S, D), F32),
            compiler_params=_params(2),
            name="out_proj",
        )(att, ysg, ymem, x, w_out[layer].astype(BF16), final_norm_g[None])
    return x
```

```python
import functools
import math

import numpy as np
import jax
import jax.numpy as jnp
from jax import lax
from jax.experimental import pallas as pl
from jax.experimental.pallas import tpu as pltpu

F32 = jnp.float32
BF16 = jnp.bfloat16

HEAD_DIM = 64
N_ATTN_HEADS = 8
ATTN_WIDTH = 512
SG_WIDTH = 512
SG_CHUNK = 128
MEM_WIDTH = 512
N_MEM_HEADS = 4
MEM_HEAD_DIM = 128
DILATED_CONFIGS = ((128, 1), (512, 4), (2048, 16))
BAND_BLOCK = 64
KEY_COLS = 256
N_REL_BUCKETS = 32
REL_MAX_DISTANCE = 1024
EPS = 1e-6
NEG_BIG = -1e30

LANES = 128
T_IN = 512
T_ATT = 1024
T_OUT = 512
VMEM_LIMIT = 48 * 1024 * 1024


def _silu(g):
    return g * (1.0 / (1.0 + jnp.exp(-g)))


def _gelu_tanh(x):
    c = math.sqrt(2.0 / math.pi)
    return x * (0.5 * (1.0 + jnp.tanh(c * (x + 0.044715 * (x * x * x)))))


def _dot(a, b):
    return jnp.dot(a, b, preferred_element_type=F32)


def _dot_nt(a, b):
    return lax.dot_general(a, b, (((1,), (1,)), ((), ())), preferred_element_type=F32)


def _mem_kv_kernel(mem_ref, g_ref, w_ref, km_ref, vm_ref):
    m = mem_ref[0]
    h = (m * lax.rsqrt(jnp.mean(m * m, axis=-1, keepdims=True) + EPS)) * g_ref[...]
    kv = _dot(h.astype(BF16), w_ref[...])
    km_ref[0] = kv[:, :MEM_WIDTH].astype(BF16)
    vm_ref[0] = kv[:, MEM_WIDTH:].astype(BF16)


def _in_proj_kernel(x_ref, ng_ref, w_ref, lng_ref, lnb_ref, sgw_ref, sgb_ref, km_ref, vm_ref,
                    q1_ref, q4_ref, q16_ref, k1_ref, k4_ref, k16_ref, v1_ref, v4_ref, v16_ref,
                    gatt_ref, ysg_ref, ymem_ref, scr_ref, scr4_ref):
    t = x_ref.shape[1]
    x = x_ref[0]
    h = (x * lax.rsqrt(jnp.mean(x * x, axis=-1, keepdims=True) + EPS)) * ng_ref[...]
    hb = h.astype(BF16)

    seg_w = ATTN_WIDTH

    def mm(i):
        return _dot(hb, w_ref[:, i * seg_w:(i + 1) * seg_w])

    def put_qkv(seg, p, outs):
        if seg == 0:
            p = p * (HEAD_DIM ** -0.5)
        o1, o4, o16 = outs
        o1[0] = p.astype(BF16)
        for c in range(ATTN_WIDTH // LANES):
            cols = slice(c * LANES, (c + 1) * LANES)
            scr_ref[seg, c] = p[:, cols]
            for r in range(4):
                a = scr_ref[seg, c, pl.ds(r, t // 4, stride=4), :]
                o4[0, r, :, cols] = a.astype(BF16)
                scr4_ref[seg, c, r] = a
                for r2 in range(4):
                    o16[0, 4 * r2 + r, :, cols] = (
                        scr4_ref[seg, c, r, pl.ds(r2, t // 16, stride=4), :].astype(BF16))

    p_q = mm(0)
    p_k = mm(1)
    put_qkv(0, p_q, (q1_ref, q4_ref, q16_ref))
    p_v = mm(2)
    put_qkv(1, p_k, (k1_ref, k4_ref, k16_ref))
    p_ga = mm(3)
    put_qkv(2, p_v, (v1_ref, v4_ref, v16_ref))
    p_u = mm(4)
    gatt_ref[0] = _silu(p_ga)
    p_vv = mm(5)
    u = _gelu_tanh(p_u)
    p_gs = mm(6)

    vv = _gelu_tanh(p_vv)
    mu = jnp.mean(vv, axis=-1, keepdims=True)
    var = jnp.mean(jnp.square(vv - mu), axis=-1, keepdims=True)
    vv = ((vv - mu) * lax.rsqrt(var + EPS)) * lng_ref[...] + lnb_ref[...]
    p_qm = mm(7)
    lo_half = lax.broadcasted_iota(jnp.int32, (SG_CHUNK, LANES), 1) < HEAD_DIM
    tiles = [(c, p) for c in range(t // SG_CHUNK) for p in range(SG_WIDTH // LANES)]
    mixed = []
    for c, p in tiles:
        vp = vv[c * SG_CHUNK:(c + 1) * SG_CHUNK, p * LANES:(p + 1) * LANES]
        rhs = jnp.concatenate([jnp.where(lo_half, vp, 0.0).astype(BF16),
                               jnp.where(lo_half, 0.0, vp).astype(BF16)], axis=0)
        mixed.append(_dot(sgw_ref[p], rhs))
    gs = _silu(p_gs)
    p_gm = mm(8)

    qm = p_qm.astype(BF16)
    heads = [slice(hd * MEM_HEAD_DIM, (hd + 1) * MEM_HEAD_DIM) for hd in range(N_MEM_HEADS)]
    s_mem = [_dot_nt(qm[:, cols], km_ref[0, :, cols]) for cols in heads]

    for (c, p), mx in zip(tiles, mixed):
        rows = slice(c * SG_CHUNK, (c + 1) * SG_CHUNK)
        cols = slice(p * LANES, (p + 1) * LANES)
        ysg_ref[0, rows, cols] = ((u[rows, cols] * (mx + sgb_ref[:, cols])) * gs[rows, cols]).astype(BF16)

    gm = _silu(p_gm)
    for cols, s in zip(heads, s_mem):
        s = s * (MEM_HEAD_DIM ** -0.5)
        m = jnp.max(s, axis=-1, keepdims=True)
        p = jnp.exp(s - m)
        l = jnp.sum(p, axis=-1, keepdims=True)
        o = _dot(p.astype(BF16), vm_ref[0, :, cols]) / l
        ymem_ref[0, :, cols] = (o * gm[:, cols]).astype(BF16)


def _dilated_kernel(q1, k1p, k1c, k1n, v1p, v1c, v1n,
                    q4, k4p, k4c, k4n, v4p, v4c, v4n,
                    q16, k16p, k16c, k16n, v16p, v16c, v16n,
                    bias_ref, gatt_ref, out_ref, kbuf, vbuf, m_scr, l_scr, a_scr, s_scr, b_scr):
    i = pl.program_id(1)
    nt = pl.num_programs(1)
    blk = BAND_BLOCK
    nkeys = KEY_COLS
    lo_half = lax.broadcasted_iota(jnp.int32, (blk, LANES), 1) < HEAD_DIM
    col = lax.broadcasted_iota(jnp.int32, (blk, nkeys), 1)
    pen_lo = jnp.where(jnp.logical_and(i == 0, col < blk), NEG_BIG, 0.0)
    pen_hi = jnp.where(jnp.logical_and(i == nt - 1, col >= 2 * blk), NEG_BIG, 0.0)
    cfgs = ((1, q1, (k1p, k1c, k1n), (v1p, v1c, v1n)),
            (4, q4, (k4p, k4c, k4n), (v4p, v4c, v4n)),
            (16, q16, (k16p, k16c, k16n), (v16p, v16c, v16n)))

    for ci, (d, q_ref, ks, vs) in enumerate(cfgs):
        mt = T_ATT // d
        nb = mt // blk
        rr = mt + 2 * blk
        for r in range(d):
            for buf, (prv, cur, nxt) in ((kbuf, ks), (vbuf, vs)):
                buf[r * rr:r * rr + blk, :] = prv[0, r]
                buf[r * rr + blk:r * rr + blk + mt, :] = cur[0, r]
                buf[r * rr + blk + mt:(r + 1) * rr, :] = nxt[0, r]
        for buf in (kbuf, vbuf):
            buf[d * rr:d * rr + blk, :] = jnp.zeros((blk, LANES), BF16)
        for hh in range(2):
            b = bias_ref[ci, hh]
            b_scr[0, hh] = b
            b_scr[1, hh] = b + pen_lo
            b_scr[2, hh] = b + pen_hi
            b_scr[3, hh] = (b + pen_lo) + pen_hi

        blocks = [(r, j) for r in range(d) for j in range(nb)]
        maxes = []
        for n, (r, j) in enumerate(blocks):
            base = r * rr + j * blk
            qb = q_ref[0, r, j * blk:(j + 1) * blk, :]
            zero = jnp.zeros_like(qb)
            qq = jnp.concatenate([jnp.where(lo_half, qb, zero), jnp.where(lo_half, zero, qb)], axis=0)
            s = _dot_nt(qq, kbuf[base:base + nkeys, :])
            variant = 3 if nb == 1 else 1 if j == 0 else 2 if j == nb - 1 else 0
            mx = []
            for hh in range(2):
                sh = s[hh * blk:(hh + 1) * blk] + b_scr[variant, hh]
                s_scr[n, hh * blk:(hh + 1) * blk, :] = sh
                mx.append(jnp.max(sh, axis=-1, keepdims=True))
            maxes.append(mx)

        for n, (r, j) in enumerate(blocks):
            base = r * rr + j * blk
            idx = pl.ds(j * (blk * d) + r, blk, stride=d) if d > 1 else pl.ds(j * blk, blk)
            ls, ps = [], []
            for hh in range(2):
                ph = jnp.exp(s_scr[n, hh * blk:(hh + 1) * blk, :] - maxes[n][hh])
                ls.append(jnp.sum(ph, axis=-1, keepdims=True))
                ps.append(ph.astype(BF16))
            o = _dot(jnp.concatenate(ps, axis=0), vbuf[base:base + nkeys, :])
            mb = jnp.where(lo_half, maxes[n][0], maxes[n][1])
            lb = jnp.where(lo_half, ls[0], ls[1])
            acc = jnp.where(lo_half, o[:blk], o[blk:])
            if ci > 0:
                mo = m_scr[idx, :]
                mn = jnp.maximum(mo, mb)
                wa = jnp.exp(mo - mn)
                wb = jnp.exp(mb - mn)
                mb, lb, acc = mn, wa * l_scr[idx, :] + wb * lb, wa * a_scr[idx, :] + wb * acc
            m_scr[idx, :] = mb
            l_scr[idx, :] = lb
            a_scr[idx, :] = acc

    out_ref[0] = ((a_scr[...] / l_scr[...]) * gatt_ref[0]).astype(BF16)


def _out_proj_kernel(att_ref, ysg_ref, ymem_ref, x_ref, w_ref, g_ref, out_ref):
    y = _dot(att_ref[0], w_ref[0:ATTN_WIDTH, :])
    y = y + _dot(ysg_ref[0], w_ref[ATTN_WIDTH:ATTN_WIDTH + SG_WIDTH, :])
    y = y + _dot(ymem_ref[0], w_ref[ATTN_WIDTH + SG_WIDTH:, :])
    z = x_ref[0] + y
    out_ref[0] = (z * lax.rsqrt(jnp.mean(z * z, axis=-1, keepdims=True) + EPS)) * g_ref[...]


def _t5_bucket(rel):
    half = N_REL_BUCKETS // 2
    max_exact = half // 2
    ret = (rel > 0).astype(np.int32) * half
    n = np.abs(rel)
    large = max_exact + (np.log(np.maximum(n, 1).astype(np.float32) / max_exact)
                         / math.log(REL_MAX_DISTANCE / max_exact)
                         * (half - max_exact)).astype(np.int32)
    large = np.minimum(large, half - 1)
    return (ret + np.where(n < max_exact, n, large)).astype(np.int32)


def _bias_tables(rel_bias):
    blk = BAND_BLOCK
    qi = np.arange(blk)[:, None]
    kj = np.arange(KEY_COLS)[None, :]
    rel = kj - blk - qi
    n_cfg = len(DILATED_CONFIGS)
    onehot = np.zeros((n_cfg, blk * KEY_COLS, N_REL_BUCKETS), np.float32)
    outside = np.zeros((n_cfg, 1, blk, KEY_COLS), np.float32)
    for c, (window, d) in enumerate(DILATED_CONFIGS):
        band = np.logical_and(np.abs(rel) <= window // (2 * d), kj < 3 * blk)
        onehot[c, np.arange(rel.size), _t5_bucket(rel * d).reshape(-1)] = band.reshape(-1)
        outside[c, 0] = np.where(band, 0.0, NEG_BIG)
    tab = jnp.einsum("cnb,bh->chn", onehot, rel_bias.astype(F32), precision=lax.Precision.HIGHEST)
    return tab.reshape(n_cfg, -1, blk, KEY_COLS) + outside


def _const_spec(shape):
    n = len(shape)
    return pl.BlockSpec(shape, lambda *_: (0,) * n)


def _params(n_grid):
    return pltpu.CompilerParams(dimension_semantics=("arbitrary",) * n_grid,
                                vmem_limit_bytes=VMEM_LIMIT)


def kernel(x, mem, norm_g, mem_norm_g, w_in, sg_ln_g, sg_ln_b, sg_w, sg_b, w_mem_kv, w_out,
           rel_bias, final_norm_g):
    B, S, D = x.shape
    n_mem = mem.shape[1]
    assert w_in.shape[0] == 1, "single-layer problem"
    bias = _bias_tables(rel_bias)
    n_pairs = ATTN_WIDTH // LANES

    for layer in range(1):
        km, vm = pl.pallas_call(
            _mem_kv_kernel,
            grid=(B,),
            in_specs=[pl.BlockSpec((1, n_mem, D), lambda b: (b, 0, 0)),
                      _const_spec((1, D)),
                      _const_spec((D, 2 * MEM_WIDTH))],
            out_specs=[pl.BlockSpec((1, n_mem, MEM_WIDTH), lambda b: (b, 0, 0))] * 2,
            out_shape=[jax.ShapeDtypeStruct((B, n_mem, MEM_WIDTH), BF16)] * 2,
            compiler_params=_params(1),
            name="mem_kv",
        )(mem, mem_norm_g[layer][None], w_mem_kv[layer].astype(BF16))

        t = T_IN
        sgw_pairs = jnp.concatenate([sg_w[layer, 0::2], sg_w[layer, 1::2]], axis=-1).astype(BF16)
        sgb_full = jnp.repeat(jnp.transpose(sg_b[layer]), HEAD_DIM, axis=1)
        tok = lambda w: pl.BlockSpec((1, t, w), lambda b, i: (b, i, 0))
        perm = lambda d: pl.BlockSpec((1, d, t // d, ATTN_WIDTH), lambda b, i: (b, 0, i, 0))
        qkv_shapes = []
        qkv_specs = []
        for _ in range(3):
            qkv_shapes += [jax.ShapeDtypeStruct((B, S, ATTN_WIDTH), BF16),
                           jax.ShapeDtypeStruct((B, 4, S // 4, ATTN_WIDTH), BF16),
                           jax.ShapeDtypeStruct((B, 16, S // 16, ATTN_WIDTH), BF16)]
            qkv_specs += [tok(ATTN_WIDTH), perm(4), perm(16)]
        outs = pl.pallas_call(
            _in_proj_kernel,
            grid=(B, S // t),
            in_specs=[tok(D),
                      _const_spec((1, D)),
                      _const_spec((D, w_in.shape[2])),
                      _const_spec((1, SG_WIDTH)),
                      _const_spec((1, SG_WIDTH)),
                      _const_spec((SG_WIDTH // LANES, SG_CHUNK, 2 * SG_CHUNK)),
                      _const_spec((SG_CHUNK, SG_WIDTH)),
                      pl.BlockSpec((1, n_mem, MEM_WIDTH), lambda b, i: (b, 0, 0)),
                      pl.BlockSpec((1, n_mem, MEM_WIDTH), lambda b, i: (b, 0, 0))],
            out_specs=qkv_specs + [tok(ATTN_WIDTH), tok(SG_WIDTH), tok(MEM_WIDTH)],
            out_shape=qkv_shapes + [jax.ShapeDtypeStruct((B, S, ATTN_WIDTH), F32),
                                    jax.ShapeDtypeStruct((B, S, SG_WIDTH), BF16),
                                    jax.ShapeDtypeStruct((B, S, MEM_WIDTH), BF16)],
            scratch_shapes=[pltpu.VMEM((3, ATTN_WIDTH // LANES, t, LANES), F32),
                            pltpu.VMEM((3, ATTN_WIDTH // LANES, 4, t // 4, LANES), F32)],
            compiler_params=_params(2),
            name="in_proj",
        )(x, norm_g[layer][None], w_in[layer].astype(BF16), sg_ln_g[layer][None], sg_ln_b[layer][None],
          sgw_pairs, sgb_full, km, vm)
        q1, q4, q16, k1, k4, k16, v1, v4, v16, gatt, ysg, ymem = outs

        blk = BAND_BLOCK
        att_in, att_specs = [], []
        for d, (qa, ka, va) in zip((1, 4, 16), ((q1, k1, v1), (q4, k4, v4), (q16, k16, v16))):
            mt = T_ATT // d
            nb = mt // blk
            last = S // d // blk - 1
            cur = pl.BlockSpec((1, d, mt, LANES), lambda b, i, p: (b, 0, i, p))
            prv = pl.BlockSpec((1, d, blk, LANES),
                               lambda b, i, p, nb=nb: (b, 0, jnp.maximum(i * nb - 1, 0), p))
            nxt = pl.BlockSpec((1, d, blk, LANES),
                               lambda b, i, p, nb=nb, last=last: (b, 0, jnp.minimum((i + 1) * nb, last), p))
            qa, ka, va = (a.reshape(B, d, S // d, ATTN_WIDTH) for a in (qa, ka, va))
            att_in += [qa, ka, ka, ka, va, va, va]
            att_specs += [cur, prv, cur, nxt, prv, cur, nxt]
        att = pl.pallas_call(
            _dilated_kernel,
            grid=(B, S // T_ATT, n_pairs),
            in_specs=att_specs + [
                pl.BlockSpec((len(DILATED_CONFIGS), 2, blk, KEY_COLS), lambda b, i, p: (0, p, 0, 0)),
                pl.BlockSpec((1, T_ATT, LANES), lambda b, i, p: (b, i, p))],
            out_specs=pl.BlockSpec((1, T_ATT, LANES), lambda b, i, p: (b, i, p)),
            out_shape=jax.ShapeDtypeStruct((B, S, ATTN_WIDTH), BF16),
            scratch_shapes=[pltpu.VMEM((T_ATT + 2 * blk * 16 + blk, LANES), BF16),
                            pltpu.VMEM((T_ATT + 2 * blk * 16 + blk, LANES), BF16),
                            pltpu.VMEM((T_ATT, LANES), F32),
                            pltpu.VMEM((T_ATT, LANES), F32),
                            pltpu.VMEM((T_ATT, LANES), F32),
                            pltpu.VMEM((T_ATT // blk, 2 * blk, KEY_COLS), F32),
                            pltpu.VMEM((4, 2, blk, KEY_COLS), F32)],
            compiler_params=_params(3),
            name="dilated",
        )(*att_in, bias, gatt)

        t = T_OUT
        tok = lambda w: pl.BlockSpec((1, t, w), lambda b, i: (b, i, 0))
        x = pl.pallas_call(
            _out_proj_kernel,
            grid=(B, S // t),
            in_specs=[tok(ATTN_WIDTH), tok(SG_WIDTH), tok(MEM_WIDTH), tok(D),
                      _const_spec((w_out.shape[1], D)),
                      _const_spec((1, D))],
            out_specs=tok(D),
            out_shape=jax.ShapeDtypeStruct((B, S, D), F32),
            compiler_params=_params(2),
            name="out_proj",
        )(att, ysg, ymem, x, w_out[layer].astype(BF16), final_norm_g[None])
    return x
```
